```python
import math
import jax, jax.numpy as jnp
from jax import lax
import numpy as np

D_MODEL = 1024
BATCH = 4
SEQ = 8192
DEPTH = 1

MEM_LEN = 256
HEAD_DIM = 64
NSA_HEADS = 8
NSA_KV_HEADS = 2
NSA_GROUP = NSA_HEADS // NSA_KV_HEADS
SB_HEADS = 8
MIX_WIDTH = (NSA_HEADS + SB_HEADS) * HEAD_DIM
CMP_LEN = 32
CMP_STRIDE = 16
CMP_HIDDEN = 256
SEL_BLOCK = 64
SEL_TOPK = 16
WINDOW = 512
Q_BLOCK = 128
ROT_DIM = HEAD_DIM // 4
ROPE_THETA = 500000.0
MEM_HEADS = 4
D_FF = 2816
NORM_EPS = 1e-6
NEG = -1e30
FORCE_SCORE = 1e4

KV_W = NSA_KV_HEADS * HEAD_DIM
IN_SPLITS = [NSA_HEADS * HEAD_DIM, KV_W, KV_W, KV_W, KV_W, KV_W, KV_W, NSA_HEADS * 3,
             SB_HEADS * HEAD_DIM, SB_HEADS * HEAD_DIM, SB_HEADS * HEAD_DIM]
IN_COLS = sum(IN_SPLITS)

kernel_name = "hymba_nsa_stickbreak_macaron_memory"


def rms_norm(x, g):
    x32 = x.astype(jnp.float32)
    y = x32 * lax.rsqrt(jnp.mean(x32 * x32, axis=-1, keepdims=True) + NORM_EPS)
    return (y * g.astype(jnp.float32)).astype(x.dtype)


def rope_angles(pos):
    freqs = ROPE_THETA ** (-jnp.arange(0, ROT_DIM, 2, dtype=jnp.float32) / ROT_DIM)
    return pos.astype(jnp.float32)[..., None] * freqs


def apply_rope(x, ang):
    half = ROT_DIM // 2
    x1, x2, xp = x[..., :half], x[..., half:ROT_DIM], x[..., ROT_DIM:]
    c = jnp.cos(ang).astype(x.dtype)
    s = jnp.sin(ang).astype(x.dtype)
    return jnp.concatenate([x1 * c - x2 * s, x2 * c + x1 * s, xp], axis=-1)


def masked_softmax(s, mask):
    s = jnp.where(mask, s.astype(jnp.float32), NEG)
    m = jnp.max(s, axis=-1, keepdims=True)
    p = jnp.where(mask, jnp.exp(s - m), 0.0)
    return p / jnp.maximum(jnp.sum(p, axis=-1, keepdims=True), 1e-30)


def swiglu(h, wg, wu, wd):
    return (jax.nn.silu(h @ wg) * (h @ wu)) @ wd


def overlap_matrix(n_cmp, n_sel):
    cs = np.arange(n_cmp)[:, None] * CMP_STRIDE
    ss = np.arange(n_sel)[None, :] * SEL_BLOCK
    ov = np.minimum(cs + CMP_LEN, ss + SEL_BLOCK) - np.maximum(cs, ss)
    return jnp.asarray(np.clip(ov, 0, None).astype(np.float32) / CMP_LEN)


def compress(t, pe, w1, w2):
    B, H, T, dk = t.shape
    ratio = CMP_LEN // CMP_STRIDE
    n_chunks = T // CMP_STRIDE
    n_cmp = n_chunks - ratio + 1
    chunks = t.reshape(B, H, n_chunks, CMP_STRIDE, dk)
    blocks = jnp.concatenate([chunks[:, :, r:r + n_cmp] for r in range(ratio)], axis=3)
    blocks = blocks + pe
    flat = blocks.reshape(B, H, n_cmp, CMP_LEN * dk)
    return jax.nn.gelu(flat @ w1) @ w2


def nsa_mixer(q, k_cmp, v_cmp, k_sel, v_sel, k_win, v_win, gates, pos,
              pe_k, w1_k, w2_k, pe_v, w1_v, w2_v, g_q, g_kc, g_ks, g_kw):
    B, T = q.shape[:2]
    dk = HEAD_DIM
    scale = dk ** -0.5
    ang = rope_angles(pos)
    q = q.reshape(B, T, NSA_KV_HEADS, NSA_GROUP, dk).transpose(0, 2, 3, 1, 4)
    q = apply_rope(rms_norm(q, g_q), ang[:, None, None])

    def heads(t):
        return t.reshape(B, T, NSA_KV_HEADS, dk).transpose(0, 2, 1, 3)

    n_cmp = (T - CMP_LEN) // CMP_STRIDE + 1
    cmp_end = jnp.arange(n_cmp) * CMP_STRIDE + CMP_LEN - 1
    ang_c = rope_angles(pos[:, cmp_end])
    kc = apply_rope(rms_norm(compress(heads(k_cmp), pe_k, w1_k, w2_k), g_kc), ang_c[:, None])
    vc = compress(heads(v_cmp), pe_v, w1_v, w2_v)
    n_sel = T // SEL_BLOCK
    top_n = min(SEL_TOPK, n_sel)
    ks_blocks = apply_rope(rms_norm(heads(k_sel), g_ks), ang[:, None]).reshape(B, NSA_KV_HEADS, n_sel, SEL_BLOCK, dk)
    vs_blocks = heads(v_sel).reshape(B, NSA_KV_HEADS, n_sel, SEL_BLOCK, dk)
    w_ov = overlap_matrix(n_cmp, n_sel)
    pad = ((0, 0), (0, 0), (WINDOW, 0), (0, 0))
    kw_pad = jnp.pad(apply_rope(rms_norm(heads(k_win), g_kw), ang[:, None]), pad)
    vw_pad = jnp.pad(heads(v_win), pad)
    gates = jax.nn.sigmoid(gates.reshape(B, T, NSA_KV_HEADS, NSA_GROUP, 3).transpose(0, 2, 3, 1, 4))

    bi = jnp.arange(B)[:, None, None, None]
    hi = jnp.arange(NSA_KV_HEADS)[None, :, None, None]
    jblk = jnp.arange(n_sel)

    def block_fn(i):
        s0 = i * Q_BLOCK
        tq = s0 + jnp.arange(Q_BLOCK)
        qb = lax.dynamic_slice_in_dim(q, s0, Q_BLOCK, axis=3)
        gb = lax.dynamic_slice_in_dim(gates, s0, Q_BLOCK, axis=3)
        m_c = cmp_end[None, :] <= tq[:, None]
        p_c = masked_softmax(jnp.einsum('bhgqd,bhnd->bhgqn', qb, kc) * scale, m_c)
        o_c = jnp.einsum('bhgqn,bhnd->bhgqd', p_c, vc)
        imp = jnp.einsum('bhgqn,nj->bhqj', p_c, w_ov)
        cur = tq // SEL_BLOCK
        valid = jblk[None, :] * SEL_BLOCK <= tq[:, None]
        forced = (jblk[None, :] == 0) | (jblk[None, :] == cur[:, None]) | (jblk[None, :] == cur[:, None] - 1)
        score = jnp.where(valid & forced, FORCE_SCORE, jnp.where(valid, imp, -1.0))
        _, idx = lax.top_k(score, top_n)
        ksb = ks_blocks[bi, hi, idx].reshape(B, NSA_KV_HEADS, Q_BLOCK, top_n * SEL_BLOCK, dk)
        vsb = vs_blocks[bi, hi, idx].reshape(B, NSA_KV_HEADS, Q_BLOCK, top_n * SEL_BLOCK, dk)
        kpos = (idx[..., None] * SEL_BLOCK + jnp.arange(SEL_BLOCK)).reshape(B, NSA_KV_HEADS, Q_BLOCK, -1)
        m_s = (kpos <= tq[:, None])[:, :, None]
        p_s = masked_softmax(jnp.einsum('bhgqd,bhqkd->bhgqk', qb, ksb) * scale, m_s)
        o_s = jnp.einsum('bhgqk,bhqkd->bhgqd', p_s, vsb)
        kwb = lax.dynamic_slice_in_dim(kw_pad, s0, Q_BLOCK + WINDOW, axis=2)
        vwb = lax.dynamic_slice_in_dim(vw_pad, s0, Q_BLOCK + WINDOW, axis=2)
        kpos_w = s0 - WINDOW + jnp.arange(Q_BLOCK + WINDOW)
        diff = tq[:, None] - kpos_w[None, :]
        m_w = (kpos_w[None, :] >= 0) & (diff >= 0) & (diff < WINDOW)
        p_w = masked_softmax(jnp.einsum('bhgqd,bhkd->bhgqk', qb, kwb) * scale, m_w)
        o_w = jnp.einsum('bhgqk,bhkd->bhgqd', p_w, vwb)
        o = gb[..., 0:1] * o_c + gb[..., 1:2] * o_s + gb[..., 2:3] * o_w
        return o.astype(q.dtype)

    out = lax.map(block_fn, jnp.arange(T // Q_BLOCK))
    return out.transpose(1, 0, 4, 2, 3, 5).reshape(B, T, NSA_HEADS * dk)


def stick_breaking(q, k, v):
    B, T = q.shape[:2]
    dk = HEAD_DIM
    scale = dk ** -0.5

    def heads(t):
        return t.reshape(B, T, SB_HEADS, dk).transpose(0, 2, 1, 3)

    q, k, v = heads(q), heads(k), heads(v)
    kpos = jnp.arange(T)

    def block_fn(i):
        s0 = i * Q_BLOCK
        tq = s0 + jnp.arange(Q_BLOCK)
        qb = lax.dynamic_slice_in_dim(q, s0, Q_BLOCK, axis=2)
        z = jnp.einsum('bhqd,bhkd->bhqk', qb, k).astype(jnp.float32) * scale
        mask = kpos[None, :] < tq[:, None]
        log_rem = jnp.where(mask, jax.nn.log_sigmoid(-z), 0.0)
        suffix = lax.cumsum(log_rem, axis=3, reverse=True) - log_rem
        a = jnp.where(mask, jnp.exp(jax.nn.log_sigmoid(z) + suffix), 0.0)
        return jnp.einsum('bhqk,bhkd->bhqd', a, v).astype(q.dtype)

    out = lax.map(block_fn, jnp.arange(T // Q_BLOCK))
    return out.transpose(1, 0, 3, 2, 4).reshape(B, T, SB_HEADS * dk)


def memory_cross_attention(h, mem_n, wq, wk, wv, wo, g_q, g_k):
    B, T = h.shape[:2]
    M = mem_n.shape[1]
    q = rms_norm((h @ wq).reshape(B, T, MEM_HEADS, HEAD_DIM), g_q)
    k = rms_norm((mem_n @ wk).reshape(B, M, MEM_HEADS, HEAD_DIM), g_k)
    v = (mem_n @ wv).reshape(B, M, MEM_HEADS, HEAD_DIM)
    s = jnp.einsum('bqhd,bkhd->bhqk', q, k).astype(jnp.float32) * HEAD_DIM ** -0.5
    p = jax.nn.softmax(s, axis=-1)
    o = jnp.einsum('bhqk,bkhd->bqhd', p, v).astype(h.dtype).reshape(B, T, MEM_HEADS * HEAD_DIM)
    return o @ wo


def setup_inputs(seed: int = 0) -> dict:
    key = jax.random.key(seed)
    ks = iter(jax.random.split(key, 40))

    def dense(shape, fan_in):
        return jax.random.normal(next(ks), (DEPTH,) + shape, jnp.float32) * fan_in ** -0.5

    def gain(n):
        return 1.0 + 0.05 * jax.random.normal(next(ks), (DEPTH, n), jnp.float32)

    x = jax.random.normal(next(ks), (BATCH, SEQ, D_MODEL), jnp.float32)
    mem = jax.random.normal(next(ks), (BATCH, MEM_LEN, D_MODEL), jnp.float32)
    start = jax.random.randint(next(ks), (BATCH, 1), 0, 4096, dtype=jnp.int32)
    positions = (start + jnp.arange(SEQ, dtype=jnp.int32)[None, :]).astype(jnp.int32)
    return {
        "x": x, "mem": mem, "positions": positions,
        "ffn1_norm": gain(D_MODEL),
        "ffn1_wg": dense((D_MODEL, D_FF), D_MODEL),
        "ffn1_wu": dense((D_MODEL, D_FF), D_MODEL),
        "ffn1_wd": dense((D_FF, D_MODEL), D_FF),
        "mix_norm": gain(D_MODEL),
        "w_in": dense((D_MODEL, IN_COLS), D_MODEL),
        "nsa_q_norm": gain(HEAD_DIM),
        "nsa_kc_norm": gain(HEAD_DIM),
        "nsa_ks_norm": gain(HEAD_DIM),
        "nsa_kw_norm": gain(HEAD_DIM),
        "cmp_pos_k": 0.1 * jax.random.normal(next(ks), (DEPTH, CMP_LEN, HEAD_DIM), jnp.float32),
        "cmp_w1_k": dense((CMP_LEN * HEAD_DIM, CMP_HIDDEN), CMP_LEN * HEAD_DIM),
        "cmp_w2_k": dense((CMP_HIDDEN, HEAD_DIM), CMP_HIDDEN),
        "cmp_pos_v": 0.1 * jax.random.normal(next(ks), (DEPTH, CMP_LEN, HEAD_DIM), jnp.float32),
        "cmp_w1_v": dense((CMP_LEN * HEAD_DIM, CMP_HIDDEN), CMP_LEN * HEAD_DIM),
        "cmp_w2_v": dense((CMP_HIDDEN, HEAD_DIM), CMP_HIDDEN),
        "w_out": dense((MIX_WIDTH, D_MODEL), MIX_WIDTH),
        "mem_x_norm": gain(D_MODEL),
        "mem_kv_norm": gain(D_MODEL),
        "mem_wq": dense((D_MODEL, MEM_HEADS * HEAD_DIM), D_MODEL),
        "mem_wk": dense((D_MODEL, MEM_HEADS * HEAD_DIM), D_MODEL),
        "mem_wv": dense((D_MODEL, MEM_HEADS * HEAD_DIM), D_MODEL),
        "mem_q_norm": gain(HEAD_DIM),
        "mem_k_norm": gain(HEAD_DIM),
        "mem_wo": dense((MEM_HEADS * HEAD_DIM, D_MODEL), MEM_HEADS * HEAD_DIM),
        "ffn2_norm": gain(D_MODEL),
        "ffn2_wg": dense((D_MODEL, D_FF), D_MODEL),
        "ffn2_wu": dense((D_MODEL, D_FF), D_MODEL),
        "ffn2_wd": dense((D_FF, D_MODEL), D_FF),
    }


def reference(x, mem, positions, ffn1_norm, ffn1_wg, ffn1_wu, ffn1_wd, mix_norm, w_in,
              nsa_q_norm, nsa_kc_norm, nsa_ks_norm, nsa_kw_norm,
              cmp_pos_k, cmp_w1_k, cmp_w2_k, cmp_pos_v, cmp_w1_v, cmp_w2_v, w_out,
              mem_x_norm, mem_kv_norm, mem_wq, mem_wk, mem_wv, mem_q_norm, mem_k_norm, mem_wo,
              ffn2_norm, ffn2_wg, ffn2_wu, ffn2_wd):
    split_at = [int(c) for c in np.cumsum(IN_SPLITS)[:-1]]
    for l in range(DEPTH):
        x = x + 0.5 * swiglu(rms_norm(x, ffn1_norm[l]), ffn1_wg[l], ffn1_wu[l], ffn1_wd[l])
        h = rms_norm(x, mix_norm[l])
        (q_n, kc, vc, ksl, vsl, kwn, vwn, gts, q_s, k_s, v_s) = jnp.split(h @ w_in[l], split_at, axis=-1)
        o_nsa = nsa_mixer(q_n, kc, vc, ksl, vsl, kwn, vwn, gts, positions,
                          cmp_pos_k[l], cmp_w1_k[l], cmp_w2_k[l], cmp_pos_v[l], cmp_w1_v[l], cmp_w2_v[l],
                          nsa_q_norm[l], nsa_kc_norm[l], nsa_ks_norm[l], nsa_kw_norm[l])
        o_sb = stick_breaking(q_s, k_s, v_s)
        x = x + jnp.concatenate([o_nsa, o_sb], axis=-1) @ w_out[l]
        x = x + memory_cross_attention(rms_norm(x, mem_x_norm[l]), rms_norm(mem, mem_kv_norm[l]),
                                       mem_wq[l], mem_wk[l], mem_wv[l], mem_wo[l],
                                       mem_q_norm[l], mem_k_norm[l])
        x = x + 0.5 * swiglu(rms_norm(x, ffn2_norm[l]), ffn2_wg[l], ffn2_wu[l], ffn2_wd[l])
    return x
```

```python
import functools
import math

import jax
import jax.numpy as jnp
import numpy as np
from jax import lax
from jax.experimental import pallas as pl
from jax.experimental.pallas import tpu as pltpu

HEAD_DIM = 64
NSA_HEADS = 8
NSA_KV_HEADS = 2
NSA_GROUP = NSA_HEADS // NSA_KV_HEADS
SB_HEADS = 8
CMP_LEN = 32
CMP_STRIDE = 16
CMP_HIDDEN = 256
SEL_BLOCK = 64
SEL_TOPK = 16
WINDOW = 512
ROT_DIM = HEAD_DIM // 4
ROPE_THETA = 500000.0
MEM_HEADS = 4
NORM_EPS = 1e-6
NEG = -1e30
FORCE_SCORE = 1e4
LANES = 128
GATE_PAD = 16

VMEM_LIMIT = 56 * 1024 * 1024

F32 = jnp.float32
BF16 = jnp.bfloat16


def _cparams(sem):
    return pltpu.CompilerParams(dimension_semantics=sem, vmem_limit_bytes=VMEM_LIMIT)


def _dot(a, b):
    return jnp.dot(a, b, preferred_element_type=F32)


def _dot_nt(a, b):
    return lax.dot_general(a, b, (((1,), (1,)), ((), ())), preferred_element_type=F32)


def _split_bf16(x):
    hi = x.astype(BF16)
    lo = (x - hi.astype(F32)).astype(BF16)
    return hi, lo


def _rms_rows(x, g):
    return x * lax.rsqrt(jnp.mean(x * x, axis=-1, keepdims=True) + NORM_EPS) * g


def _slab_head_norm(x, g2):
    lane = lax.broadcasted_iota(jnp.int32, x.shape, 1)
    lo = lane < HEAD_DIM
    sq = x * x
    s_lo = jnp.sum(jnp.where(lo, sq, 0.0), axis=-1, keepdims=True)
    s_hi = jnp.sum(jnp.where(lo, 0.0, sq), axis=-1, keepdims=True)
    ms = jnp.where(lo, s_lo, s_hi) * (1.0 / HEAD_DIM)
    return x * lax.rsqrt(ms + NORM_EPS) * g2


def _rope_tables(pos_col, freq_lane):
    ang = pos_col.astype(F32) * freq_lane
    c = jnp.cos(ang)
    s = jnp.sin(ang)
    lane = lax.broadcasted_iota(jnp.int32, ang.shape, 1) % HEAD_DIM
    half = ROT_DIM // 2
    s_up = jnp.where(lane < half, -s, 0.0)
    s_dn = jnp.where((lane >= half) & (lane < ROT_DIM), s, 0.0)
    return c, s_up, s_dn


def _slab_rope(x, tabs):
    c, s_up, s_dn = tabs
    half = ROT_DIM // 2
    x_up = pltpu.roll(x, LANES - half, 1)
    x_dn = pltpu.roll(x, half, 1)
    return x * c + x_up * s_up + x_dn * s_dn


def _ffn_body(x_ref, g_ref, wg_ref, wu_ref, wd_ref, o_ref):
    x = x_ref[...]
    h = _rms_rows(x, g_ref[...]).astype(BF16)
    gate = _dot(h, wg_ref[...])
    up = _dot(h, wu_ref[...])
    act = (gate * jax.nn.sigmoid(gate) * up).astype(BF16)
    o_ref[...] = x + 0.5 * _dot(act, wd_ref[...])


def _ffn(x2d, g, wg, wu, wd, tm):
    n, d = x2d.shape
    dff = wg.shape[1]
    const = lambda i: (0, 0)
    return pl.pallas_call(
        _ffn_body,
        grid=(n // tm,),
        in_specs=[
            pl.BlockSpec((tm, d), lambda i: (i, 0)),
            pl.BlockSpec((1, d), const),
            pl.BlockSpec((d, dff), const, pipeline_mode=pl.Buffered(1)),
            pl.BlockSpec((d, dff), const, pipeline_mode=pl.Buffered(1)),
            pl.BlockSpec((dff, d), const, pipeline_mode=pl.Buffered(1)),
        ],
        out_specs=pl.BlockSpec((tm, d), lambda i: (i, 0)),
        out_shape=jax.ShapeDtypeStruct((n, d), F32),
        compiler_params=_cparams(("parallel",)),
        name="ffn",
    )(x2d, g, wg, wu, wd)


N_NSA_SLABS = 10
N_SB_SLABS = 12


def _store_slab(ref, slab_idx, y):
    ref[0, 2 * slab_idx] = y[:, :HEAD_DIM].astype(ref.dtype)
    ref[0, 2 * slab_idx + 1] = y[:, HEAD_DIM:].astype(ref.dtype)


def _proj_body(x_ref, pos_ref, g_ref, wn_ref, wgt_ref, ws_ref, gq_ref, gks_ref, gkw_ref, freq_ref,
               a_ref, c_ref, s_ref, gt_ref):
    h = _rms_rows(x_ref[...], g_ref[...]).astype(BF16)
    tabs = _rope_tables(pos_ref[0], freq_ref[...])
    scale = HEAD_DIM ** -0.5
    cols = _dot(h, wn_ref[...])
    for j in range(N_NSA_SLABS):
        y = cols[:, LANES * j:LANES * (j + 1)]
        if j < 4:
            y = _slab_rope(_slab_head_norm(y, gq_ref[...]), tabs) * scale
        elif j == 6:
            y = _slab_rope(_slab_head_norm(y, gks_ref[...]), tabs)
        elif j == 8:
            y = _slab_rope(_slab_head_norm(y, gkw_ref[...]), tabs)
        if j == 4 or j == 5:
            _store_slab(c_ref, j - 4, y)
        else:
            _store_slab(a_ref, j if j < 4 else j - 2, y)
    gt = jax.nn.sigmoid(_dot(h, wgt_ref[...]))
    gt_ref[0, 0] = gt[:, :GATE_PAD]
    gt_ref[0, 1] = gt[:, GATE_PAD:2 * GATE_PAD]
    cols = _dot(h, ws_ref[...])
    for j in range(N_SB_SLABS):
        y = cols[:, LANES * j:LANES * (j + 1)]
        if j < 4:
            y = y * scale
        _store_slab(s_ref, j, y)


def _proj(x, pos3, g, wn, wgt, ws, gq2, gks2, gkw2, freq_lane, tm):
    b, t, d = x.shape
    const = lambda bi, i: (0, 0)
    tok4 = lambda bi, i: (bi, 0, i, 0)
    return pl.pallas_call(
        _proj_body,
        grid=(b, t // tm),
        in_specs=[
            pl.BlockSpec((None, tm, d), lambda bi, i: (bi, i, 0)),
            pl.BlockSpec((1, tm, 1), lambda bi, i: (bi, i, 0)),
            pl.BlockSpec((1, d), const),
            pl.BlockSpec(wn.shape, const, pipeline_mode=pl.Buffered(1)),
            pl.BlockSpec(wgt.shape, const, pipeline_mode=pl.Buffered(1)),
            pl.BlockSpec(ws.shape, const, pipeline_mode=pl.Buffered(1)),
            pl.BlockSpec((1, LANES), const),
            pl.BlockSpec((1, LANES), const),
            pl.BlockSpec((1, LANES), const),
            pl.BlockSpec((1, LANES), const),
        ],
        out_specs=[
            pl.BlockSpec((1, 16, tm, HEAD_DIM), tok4),
            pl.BlockSpec((1, 4, tm, HEAD_DIM), tok4),
            pl.BlockSpec((1, 24, tm, HEAD_DIM), tok4),
            pl.BlockSpec((1, 2, tm, GATE_PAD), tok4),
        ],
        out_shape=[
            jax.ShapeDtypeStruct((b, 16, t, HEAD_DIM), BF16),
            jax.ShapeDtypeStruct((b, 4, t, HEAD_DIM), BF16),
            jax.ShapeDtypeStruct((b, 24, t, HEAD_DIM), BF16),
            jax.ShapeDtypeStruct((b, 2, t, GATE_PAD), F32),
        ],
        compiler_params=_cparams(("parallel", "parallel")),
        name="proj",
    )(x, pos3, g, wn, wgt, ws, gq2, gks2, gkw2, freq_lane)


def _compress_body(c_ref, posc_ref, w1_ref, pe_ref, w2a_ref, w2b_ref, gkc_ref, freq_ref, o_ref):
    is_k = pl.program_id(1) == 0
    w1 = w1_ref[0]
    half = w1.shape[0] // 2
    bias = _dot(jnp.broadcast_to(pe_ref[0], (8, 2 * half)).astype(BF16), w1)[0:1]
    slab = None
    for hh in range(NSA_KV_HEADS):
        chunks = c_ref[0, hh]
        n_chunks = chunks.shape[0]
        a0 = _dot(chunks, w1[:half])
        a1 = _dot(chunks, w1[half:])
        hid = a0 + pltpu.roll(a1, n_chunks - 1, 0) + bias
        act = jax.nn.gelu(hid).astype(BF16)
        part = _dot(act, (w2a_ref if hh == 0 else w2b_ref)[0])
        slab = part if slab is None else slab + part
    tabs = _rope_tables(posc_ref[0], freq_ref[...])
    normed = _slab_rope(_slab_head_norm(slab, gkc_ref[...]), tabs)
    y = jnp.where(is_k, normed, slab)
    o_ref[0, 0] = y[:, :HEAD_DIM].astype(o_ref.dtype)
    o_ref[0, 1] = y[:, HEAD_DIM:].astype(o_ref.dtype)


def _compress(cflat, posc3, w1s, pes, w2a, w2b, gkc2, freq_lane):
    b, _, n_chunks, width = cflat.shape
    const = lambda bi, i: (0, 0)
    kv3 = lambda bi, i: (i, 0, 0)
    return pl.pallas_call(
        _compress_body,
        grid=(b, 2),
        in_specs=[
            pl.BlockSpec((1, 2, n_chunks, width), lambda bi, i: (bi, i, 0, 0)),
            pl.BlockSpec((1, n_chunks, 1), lambda bi, i: (bi, 0, 0)),
            pl.BlockSpec((1,) + w1s.shape[1:], kv3),
            pl.BlockSpec((1,) + pes.shape[1:], kv3),
            pl.BlockSpec((1,) + w2a.shape[1:], kv3),
            pl.BlockSpec((1,) + w2b.shape[1:], kv3),
            pl.BlockSpec((1, LANES), const),
            pl.BlockSpec((1, LANES), const),
        ],
        out_specs=pl.BlockSpec((1, 2, n_chunks, HEAD_DIM), lambda bi, i: (bi, i, 0, 0)),
        out_shape=jax.ShapeDtypeStruct((b, 4, n_chunks, HEAD_DIM), BF16),
        compiler_params=_cparams(("parallel", "parallel")),
        name="compress",
    )(cflat, posc3, w1s, pes, w2a, w2b, gkc2, freq_lane)


def _online_softmax_loop(q, k_ref, v_ref, lo, hi, tk, mask_fn):
    m_rows = q.shape[0]

    def step(kt, carry):
        m_run, l_run, acc = carry
        off = pl.multiple_of(kt * tk, tk)
        k = k_ref[pl.ds(off, tk), :]
        v = v_ref[pl.ds(off, tk), :]
        mask = mask_fn(kt)
        s = jnp.where(mask, _dot_nt(q, k), NEG)
        m_new = jnp.maximum(m_run, jnp.max(s, axis=-1, keepdims=True))
        p = jnp.where(mask, jnp.exp(s - m_new), 0.0)
        alpha = jnp.exp(m_run - m_new)
        l_new = alpha * l_run + jnp.sum(p, axis=-1, keepdims=True)
        acc = alpha * acc + _dot(p.astype(BF16), v)
        return m_new, l_new, acc

    init = (jnp.full((m_rows, 1), NEG, F32), jnp.zeros((m_rows, 1), F32), jnp.zeros((m_rows, HEAD_DIM), F32))
    _, l_run, acc = lax.fori_loop(lo, hi, step, init)
    return acc / jnp.maximum(l_run, 1e-30)


def _nsa_body(q_ref, kc_ref, vc_ref, ks_ref, vs_ref, kw_ref, vw_ref, gt_ref, wov_ref, o_ref, *, tq, tk_sel, tk_win, top_n):
    qi = pl.program_id(2)
    s0 = qi * tq
    grp = NSA_GROUP
    q = q_ref[0].reshape(grp * tq, HEAD_DIM)
    n_cmp = kc_ref.shape[2]
    n_sel = wov_ref.shape[1]

    t_row = s0 + lax.broadcasted_iota(jnp.int32, (grp, tq, 1), 1).reshape(grp * tq, 1)
    t_one = s0 + lax.broadcasted_iota(jnp.int32, (tq, 1), 0)

    sc = _dot_nt(q, kc_ref[0, 0])
    cmp_end = lax.broadcasted_iota(jnp.int32, (1, n_cmp), 1) * CMP_STRIDE + (CMP_LEN - 1)
    mask_c = cmp_end <= t_row
    sc = jnp.where(mask_c, sc, NEG)
    pc = jnp.where(mask_c, jnp.exp(sc - jnp.max(sc, axis=-1, keepdims=True)), 0.0)
    pc = pc / jnp.maximum(jnp.sum(pc, axis=-1, keepdims=True), 1e-30)
    o_c = _dot(pc.astype(BF16), vc_ref[0, 0])

    p_sum = pc[0:tq]
    for g in range(1, grp):
        p_sum = p_sum + pc[g * tq:(g + 1) * tq]
    p_hi, p_lo = _split_bf16(p_sum)
    imp = _dot(p_hi, wov_ref[...]) + _dot(p_lo, wov_ref[...])
    jblk = lax.broadcasted_iota(jnp.int32, (tq, n_sel), 1)
    cur = t_one // SEL_BLOCK
    valid = jblk * SEL_BLOCK <= t_one
    forced = (jblk == 0) | (jblk == cur) | (jblk == cur - 1)
    score = jnp.where(valid & forced, FORCE_SCORE, jnp.where(valid, imp, -1.0))
    jf = jblk.astype(F32)
    sel = jnp.zeros((tq, n_sel), F32)
    for _ in range(top_n):
        best = jnp.max(score, axis=-1, keepdims=True)
        first = jnp.min(jnp.where(score == best, jf, float(n_sel)), axis=-1, keepdims=True)
        pick = jf == first
        sel = jnp.where(pick, 1.0, sel)
        score = jnp.where(pick, -2.0, score)
    sel = jnp.where(valid, sel, 0.0).astype(BF16)

    blocks_per_tile = tk_sel // SEL_BLOCK

    def sel_mask(kt):
        rowb = lax.broadcasted_iota(jnp.int32, (n_sel, tk_sel), 0)
        colb = kt * blocks_per_tile + lax.broadcasted_iota(jnp.int32, (n_sel, tk_sel), 1) // SEL_BLOCK
        expand = (rowb == colb).astype(BF16)
        chosen = _dot(sel, expand) > 0.5
        kpos = kt * tk_sel + lax.broadcasted_iota(jnp.int32, (1, tk_sel), 1)
        ok = chosen & (kpos <= t_one)
        return jnp.broadcast_to(ok[None], (grp, tq, tk_sel)).reshape(grp * tq, tk_sel)

    n_kt_sel = (s0 + tq + tk_sel - 1) // tk_sel
    o_s = _online_softmax_loop(q, ks_ref.at[0, 0], vs_ref.at[0, 0], 0, n_kt_sel, tk_sel, sel_mask)

    def win_mask(kt):
        kpos = kt * tk_win + lax.broadcasted_iota(jnp.int32, (1, tk_win), 1)
        diff = t_row - kpos
        return (diff >= 0) & (diff < WINDOW)

    lo_win = jnp.maximum(s0 - WINDOW + 1, 0) // tk_win
    hi_win = (s0 + tq + tk_win - 1) // tk_win
    o_w = _online_softmax_loop(q, kw_ref.at[0, 0], vw_ref.at[0, 0], lo_win, hi_win, tk_win, win_mask)

    gates = gt_ref[0, 0]
    for g in range(grp):
        rows = slice(g * tq, (g + 1) * tq)
        o = (gates[:, 3 * g:3 * g + 1] * o_c[rows] + gates[:, 3 * g + 1:3 * g + 2] * o_s[rows]
             + gates[:, 3 * g + 2:3 * g + 3] * o_w[rows])
        o_ref[0, :, g * HEAD_DIM:(g + 1) * HEAD_DIM] = o.astype(o_ref.dtype)


def _nsa(a, cc, gates, wov, tq, tk_sel, tk_win):
    b, _, t, dk = a.shape
    n_cmp = cc.shape[2]
    grp = NSA_GROUP
    top_n = min(SEL_TOPK, t // SEL_BLOCK)
    head = lambda base: (lambda bi, h, i: (bi, base + h, 0, 0))
    body = functools.partial(_nsa_body, tq=tq, tk_sel=tk_sel, tk_win=tk_win, top_n=top_n)
    return pl.pallas_call(
        body,
        grid=(b, NSA_KV_HEADS, t // tq),
        in_specs=[
            pl.BlockSpec((1, grp, tq, dk), lambda bi, h, i: (bi, h, i, 0)),
            pl.BlockSpec((1, 1, n_cmp, dk), head(0)),
            pl.BlockSpec((1, 1, n_cmp, dk), head(2)),
            pl.BlockSpec((1, 1, t, dk), head(8)),
            pl.BlockSpec((1, 1, t, dk), head(10)),
            pl.BlockSpec((1, 1, t, dk), head(12)),
            pl.BlockSpec((1, 1, t, dk), head(14)),
            pl.BlockSpec((1, 1, tq, GATE_PAD), lambda bi, h, i: (bi, h, i, 0)),
            pl.BlockSpec(wov.shape, lambda bi, h, i: (0, 0)),
        ],
        out_specs=pl.BlockSpec((1, tq, grp * dk), lambda bi, h, i: (bi, i, h)),
        out_shape=jax.ShapeDtypeStruct((b, t, NSA_HEADS * dk), BF16),
        compiler_params=_cparams(("parallel", "parallel", "arbitrary")),
        name="nsa",
    )(a, cc, cc, a, a, a, a, gates, wov)


def _sb_body(q_ref, k_ref, v_ref, o_ref, *, tq):
    qi = pl.program_id(2)
    tk = tq
    rowi = lax.broadcasted_iota(jnp.int32, (tq, tk), 0)
    coli = lax.broadcasted_iota(jnp.int32, (tq, tk), 1)
    later = (rowi > coli).astype(BF16)
    strict = coli < rowi

    for hh in range(2):
        q = q_ref[0, hh]

        def tile(kb, carry, masked):
            c_run, acc = carry
            off = pl.multiple_of(kb * tk, tk)
            k = k_ref[0, hh, pl.ds(off, tk), :]
            v = v_ref[0, hh, pl.ds(off, tk), :]
            z = _dot_nt(q, k)
            sp = jnp.maximum(z, 0.0) + jnp.log1p(jnp.exp(-jnp.abs(z)))
            log_rem = jnp.where(strict, -sp, 0.0) if masked else -sp
            r_hi, r_lo = _split_bf16(log_rem)
            suffix = _dot(r_hi, later) + _dot(r_lo, later)
            e = jnp.exp(z - sp + suffix + c_run)
            a = jnp.where(strict, e, 0.0) if masked else e
            acc = acc + _dot(a.astype(BF16), v)
            c_run = c_run + jnp.sum(log_rem, axis=-1, keepdims=True)
            return c_run, acc

        carry = (jnp.zeros((tq, 1), F32), jnp.zeros((tq, HEAD_DIM), F32))
        carry = tile(qi, carry, True)
        carry = lax.fori_loop(0, qi, lambda i, c: tile(qi - 1 - i, c, False), carry)
        o_ref[0, :, hh * HEAD_DIM:(hh + 1) * HEAD_DIM] = carry[1].astype(o_ref.dtype)


def _sb(s, tq):
    b, _, t, dk = s.shape
    pair = lambda base: (lambda bi, hp, i: (bi, base + hp, 0, 0))
    return pl.pallas_call(
        functools.partial(_sb_body, tq=tq),
        grid=(b, SB_HEADS // 2, t // tq),
        in_specs=[
            pl.BlockSpec((1, 2, tq, dk), lambda bi, hp, i: (bi, hp, i, 0)),
            pl.BlockSpec((1, 2, t, dk), pair(SB_HEADS // 2)),
            pl.BlockSpec((1, 2, t, dk), pair(SB_HEADS)),
        ],
        out_specs=pl.BlockSpec((1, tq, 2 * dk), lambda bi, hp, i: (bi, i, hp)),
        out_shape=jax.ShapeDtypeStruct((b, t, SB_HEADS * dk), BF16),
        compiler_params=_cparams(("parallel", "parallel", "arbitrary")),
        name="sb",
    )(s, s, s)


def _memkv_body(m_ref, g_ref, wk_ref, wv_ref, gk_ref, k_ref, v_ref):
    mn = _rms_rows(m_ref[0], g_ref[...]).astype(BF16)
    kk = _dot(mn, wk_ref[...])
    vv = _dot(mn, wv_ref[...])
    for j in range(MEM_HEADS // 2):
        ks = _slab_head_norm(kk[:, LANES * j:LANES * (j + 1)], gk_ref[...])
        vs = vv[:, LANES * j:LANES * (j + 1)]
        _store_slab(k_ref, j, ks)
        _store_slab(v_ref, j, vs)


def _memkv(mem, g, wk, wv, gk2):
    b, m, d = mem.shape
    const = lambda bi: (0, 0)
    out = jax.ShapeDtypeStruct((b, MEM_HEADS, m, HEAD_DIM), BF16)
    return pl.pallas_call(
        _memkv_body,
        grid=(b,),
        in_specs=[
            pl.BlockSpec((1, m, d), lambda bi: (bi, 0, 0)),
            pl.BlockSpec((1, d), const),
            pl.BlockSpec(wk.shape, const),
            pl.BlockSpec(wv.shape, const),
            pl.BlockSpec((1, LANES), const),
        ],
        out_specs=[pl.BlockSpec((1, MEM_HEADS, m, HEAD_DIM), lambda bi: (bi, 0, 0, 0))] * 2,
        out_shape=[out, out],
        compiler_params=_cparams(("parallel",)),
        name="memkv",
    )(mem, g, wk, wv, gk2)


def _outmem_body(x_ref, on_ref, os_ref, wo1_ref, wo2_ref, g_ref, wq_ref, gq_ref, mk_ref, mv_ref, wmo_ref, o_ref):
    x = x_ref[0] + _dot(on_ref[0], wo1_ref[...]) + _dot(os_ref[0], wo2_ref[...])
    h = _rms_rows(x, g_ref[...]).astype(BF16)
    qq = _dot(h, wq_ref[...])
    scale = HEAD_DIM ** -0.5
    upd = jnp.zeros_like(x)
    for j in range(MEM_HEADS // 2):
        qs = (_slab_head_norm(qq[:, LANES * j:LANES * (j + 1)], gq_ref[...]) * scale).astype(BF16)
        for hh in range(2):
            head = 2 * j + hh
            s = _dot_nt(qs[:, hh * HEAD_DIM:(hh + 1) * HEAD_DIM], mk_ref[0, head])
            p = jnp.exp(s - jnp.max(s, axis=-1, keepdims=True))
            p = p / jnp.sum(p, axis=-1, keepdims=True)
            o = _dot(p.astype(BF16), mv_ref[0, head]).astype(BF16)
            upd = upd + _dot(o, wmo_ref[head * HEAD_DIM:(head + 1) * HEAD_DIM, :])
    o_ref[0] = x + upd


def _outmem(x, o_nsa, o_sb, wo1, wo2, g, wq, gq2, mk, mv, wmo, tm):
    b, t, d = x.shape
    m = mk.shape[2]
    const = lambda bi, i: (0, 0)
    tok = lambda bi, i: (bi, i, 0)
    return pl.pallas_call(
        _outmem_body,
        grid=(b, t // tm),
        in_specs=[
            pl.BlockSpec((1, tm, d), tok),
            pl.BlockSpec((1, tm, o_nsa.shape[2]), tok),
            pl.BlockSpec((1, tm, o_sb.shape[2]), tok),
            pl.BlockSpec(wo1.shape, const),
            pl.BlockSpec(wo2.shape, const),
            pl.BlockSpec((1, d), const),
            pl.BlockSpec(wq.shape, const),
            pl.BlockSpec((1, LANES), const),
            pl.BlockSpec((1, MEM_HEADS, m, HEAD_DIM), lambda bi, i: (bi, 0, 0, 0)),
            pl.BlockSpec((1, MEM_HEADS, m, HEAD_DIM), lambda bi, i: (bi, 0, 0, 0)),
            pl.BlockSpec(wmo.shape, const),
        ],
        out_specs=pl.BlockSpec((1, tm, d), tok),
        out_shape=jax.ShapeDtypeStruct((b, t, d), F32),
        compiler_params=_cparams(("parallel", "parallel")),
        name="outmem",
    )(x, o_nsa, o_sb, wo1, wo2, g, wq, gq2, mk, mv, wmo)


def _tile2(g):
    return jnp.concatenate([g, g]).reshape(1, LANES).astype(F32)


def _overlap_matrix(n_cmp_pad, n_sel):
    n_cmp = n_cmp_pad - 1
    cs = np.arange(n_cmp_pad)[:, None] * CMP_STRIDE
    ss = np.arange(n_sel)[None, :] * SEL_BLOCK
    ov = np.minimum(cs + CMP_LEN, ss + SEL_BLOCK) - np.maximum(cs, ss)
    ov = np.clip(ov, 0, None).astype(np.float32) / CMP_LEN
    ov[n_cmp:] = 0.0
    return jnp.asarray(ov, dtype=BF16)


def _pick_tile(n, pref):
    while n % pref:
        pref //= 2
    return pref


def _layer(x, mem, positions, p):
    b, t, d = x.shape
    n_tok = b * t
    tm = _pick_tile(t, 512)

    x = _ffn(x.reshape(n_tok, d), p["ffn1_norm"].reshape(1, d), p["ffn1_wg"].astype(BF16), p["ffn1_wu"].astype(BF16),
             p["ffn1_wd"].astype(BF16), tm).reshape(b, t, d)

    w_in = p["w_in"]
    n_nsa = N_NSA_SLABS * LANES
    n_gate = NSA_HEADS * 3
    wn = w_in[:, :n_nsa].astype(BF16)
    wg_cols = w_in[:, n_nsa:n_nsa + n_gate].reshape(d, NSA_KV_HEADS, NSA_GROUP * 3)
    wgt = jnp.pad(wg_cols, ((0, 0), (0, 0), (0, GATE_PAD - NSA_GROUP * 3))).reshape(d, NSA_KV_HEADS * GATE_PAD)
    wgt = jnp.pad(wgt, ((0, 0), (0, LANES - NSA_KV_HEADS * GATE_PAD))).astype(BF16)
    ws = w_in[:, n_nsa + n_gate:].astype(BF16)
    freqs = ROPE_THETA ** (-jnp.arange(0, ROT_DIM, 2, dtype=F32) / ROT_DIM)
    half = ROT_DIM // 2
    freq_head = jnp.concatenate([freqs, freqs, jnp.zeros((HEAD_DIM - ROT_DIM,), F32)])
    freq_lane = jnp.concatenate([freq_head, freq_head]).reshape(1, LANES)
    del half
    a, c, s, gates = _proj(x, positions.reshape(b, t, 1), p["mix_norm"].reshape(1, d), wn, wgt, ws,
                           _tile2(p["nsa_q_norm"]), _tile2(p["nsa_ks_norm"]), _tile2(p["nsa_kw_norm"]), freq_lane, tm)

    n_chunks = t // CMP_STRIDE
    n_cmp = n_chunks - CMP_LEN // CMP_STRIDE + 1
    cflat = c.reshape(b, 4, n_chunks, CMP_STRIDE * HEAD_DIM)
    posc = jnp.pad(positions[:, CMP_LEN - 1::CMP_STRIDE][:, :n_cmp], ((0, 0), (0, n_chunks - n_cmp)))
    w1s = jnp.stack([p["cmp_w1_k"], p["cmp_w1_v"]]).astype(BF16)
    pes = jnp.stack([p["cmp_pos_k"].reshape(1, -1), p["cmp_pos_v"].reshape(1, -1)])
    zpad = ((0, 0), (0, 0), (0, HEAD_DIM))
    w2s = jnp.stack([p["cmp_w2_k"], p["cmp_w2_v"]])
    w2a = jnp.pad(w2s, zpad).astype(BF16)
    w2b = jnp.pad(w2s, ((0, 0), (0, 0), (HEAD_DIM, 0))).astype(BF16)
    cc = _compress(cflat, posc.reshape(b, n_chunks, 1), w1s, pes, w2a, w2b, _tile2(p["nsa_kc_norm"]), freq_lane)

    wov = _overlap_matrix(n_chunks, t // SEL_BLOCK)
    tq = _pick_tile(t, 128)
    o_nsa = _nsa(a, cc, gates, wov, tq, _pick_tile(t, 256), tq)
    o_sb = _sb(s, _pick_tile(t, 256))

    w_out = p["w_out"].astype(BF16)
    n_o = NSA_HEADS * HEAD_DIM
    mk, mv = _memkv(mem, p["mem_kv_norm"].reshape(1, d), p["mem_wk"].astype(BF16), p["mem_wv"].astype(BF16),
                    _tile2(p["mem_k_norm"]))
    x = _outmem(x, o_nsa, o_sb, w_out[:n_o], w_out[n_o:], p["mem_x_norm"].reshape(1, d), p["mem_wq"].astype(BF16),
                _tile2(p["mem_q_norm"]), mk, mv, p["mem_wo"].astype(BF16), tm)

    x = _ffn(x.reshape(n_tok, d), p["ffn2_norm"].reshape(1, d), p["ffn2_wg"].astype(BF16), p["ffn2_wu"].astype(BF16),
             p["ffn2_wd"].astype(BF16), tm).reshape(b, t, d)
    return x


def kernel(x, mem, positions, ffn1_norm, ffn1_wg, ffn1_wu, ffn1_wd, mix_norm, w_in, nsa_q_norm, nsa_kc_norm,
           nsa_ks_norm, nsa_kw_norm, cmp_pos_k, cmp_w1_k, cmp_w2_k, cmp_pos_v, cmp_w1_v, cmp_w2_v, w_out,
           mem_x_norm, mem_kv_norm, mem_wq, mem_wk, mem_wv, mem_q_norm, mem_k_norm, mem_wo,
           ffn2_norm, ffn2_wg, ffn2_wu, ffn2_wd):
    names = ("ffn1_norm", "ffn1_wg", "ffn1_wu", "ffn1_wd", "mix_norm", "w_in", "nsa_q_norm", "nsa_kc_norm",
             "nsa_ks_norm", "nsa_kw_norm", "cmp_pos_k", "cmp_w1_k", "cmp_w2_k", "cmp_pos_v", "cmp_w1_v", "cmp_w2_v",
             "w_out", "mem_x_norm", "mem_kv_norm", "mem_wq", "mem_wk", "mem_wv", "mem_q_norm", "mem_k_norm", "mem_wo",
             "ffn2_norm", "ffn2_wg", "ffn2_wu", "ffn2_wd")
    vals = (ffn1_norm, ffn1_wg, ffn1_wu, ffn1_wd, mix_norm, w_in, nsa_q_norm, nsa_kc_norm,
            nsa_ks_norm, nsa_kw_norm, cmp_pos_k, cmp_w1_k, cmp_w2_k, cmp_pos_v, cmp_w1_v, cmp_w2_v,
            w_out, mem_x_norm, mem_kv_norm, mem_wq, mem_wk, mem_wv, mem_q_norm, mem_k_norm, mem_wo,
            ffn2_norm, ffn2_wg, ffn2_wu, ffn2_wd)
    depth = ffn1_norm.shape[0]
    for l in range(depth):
        x = _layer(x, mem, positions, {n: v[l] for n, v in zip(names, vals)})
    return x
```

```python
import functools
import math

import jax
import jax.numpy as jnp
import numpy as np
from jax import lax
from jax.experimental import pallas as pl
from jax.experimental.pallas import tpu as pltpu

HEAD_DIM = 64
NSA_HEADS = 8
NSA_KV_HEADS = 2
NSA_GROUP = NSA_HEADS // NSA_KV_HEADS
SB_HEADS = 8
CMP_LEN = 32
CMP_STRIDE = 16
CMP_HIDDEN = 256
SEL_BLOCK = 64
SEL_TOPK = 16
WINDOW = 512
ROT_DIM = HEAD_DIM // 4
ROPE_THETA = 500000.0
MEM_HEADS = 4
NORM_EPS = 1e-6
NEG = -1e30
FORCE_SCORE = 1e4
LOG2E = math.log2(math.e)
LANES = 128
GATE_PAD = 16

VMEM_LIMIT = 56 * 1024 * 1024

F32 = jnp.float32
BF16 = jnp.bfloat16


def _cparams(sem):
    return pltpu.CompilerParams(dimension_semantics=sem, vmem_limit_bytes=VMEM_LIMIT)


def _dot(a, b):
    return jnp.dot(a, b, preferred_element_type=F32)


def _dot_nt(a, b):
    return lax.dot_general(a, b, (((1,), (1,)), ((), ())), preferred_element_type=F32)


def _split_bf16(x):
    hi = x.astype(BF16)
    lo = (x - hi.astype(F32)).astype(BF16)
    return hi, lo


def _rms_rows(x, g):
    return x * lax.rsqrt(jnp.mean(x * x, axis=-1, keepdims=True) + NORM_EPS) * g


def _slab_head_norm(x, g2):
    lane = lax.broadcasted_iota(jnp.int32, x.shape, 1)
    lo = lane < HEAD_DIM
    sq = x * x
    s_lo = jnp.sum(jnp.where(lo, sq, 0.0), axis=-1, keepdims=True)
    s_hi = jnp.sum(jnp.where(lo, 0.0, sq), axis=-1, keepdims=True)
    ms = jnp.where(lo, s_lo, s_hi) * (1.0 / HEAD_DIM)
    return x * lax.rsqrt(ms + NORM_EPS) * g2


def _rope_tables(pos_col, freq_lane):
    ang = pos_col.astype(F32) * freq_lane
    c = jnp.cos(ang)
    s = jnp.sin(ang)
    lane = lax.broadcasted_iota(jnp.int32, ang.shape, 1) % HEAD_DIM
    half = ROT_DIM // 2
    s_up = jnp.where(lane < half, -s, 0.0)
    s_dn = jnp.where((lane >= half) & (lane < ROT_DIM), s, 0.0)
    return c, s_up, s_dn


def _slab_rope(x, tabs):
    c, s_up, s_dn = tabs
    half = ROT_DIM // 2
    x_up = pltpu.roll(x, LANES - half, 1)
    x_dn = pltpu.roll(x, half, 1)
    return x * c + x_up * s_up + x_dn * s_dn


def _ffn_body(x_ref, g_ref, wg_ref, wu_ref, wd_ref, o_ref):
    x = x_ref[...]
    h = _rms_rows(x, g_ref[...]).astype(BF16)
    gate = _dot(h, wg_ref[...])
    up = _dot(h, wu_ref[...])
    act = (gate * jax.nn.sigmoid(gate) * up).astype(BF16)
    o_ref[...] = x + 0.5 * _dot(act, wd_ref[...])


def _ffn(x2d, g, wg, wu, wd, tm):
    n, d = x2d.shape
    dff = wg.shape[1]
    const = lambda i: (0, 0)
    return pl.pallas_call(
        _ffn_body,
        grid=(n // tm,),
        in_specs=[
            pl.BlockSpec((tm, d), lambda i: (i, 0)),
            pl.BlockSpec((1, d), const),
            pl.BlockSpec((d, dff), const, pipeline_mode=pl.Buffered(1)),
            pl.BlockSpec((d, dff), const, pipeline_mode=pl.Buffered(1)),
            pl.BlockSpec((dff, d), const, pipeline_mode=pl.Buffered(1)),
        ],
        out_specs=pl.BlockSpec((tm, d), lambda i: (i, 0)),
        out_shape=jax.ShapeDtypeStruct((n, d), F32),
        compiler_params=_cparams(("parallel",)),
        name="ffn",
    )(x2d, g, wg, wu, wd)


N_NSA_SLABS = 10
N_SB_SLABS = 12


def _store_slab(ref, slab_idx, y):
    ref[0, 2 * slab_idx] = y[:, :HEAD_DIM].astype(ref.dtype)
    ref[0, 2 * slab_idx + 1] = y[:, HEAD_DIM:].astype(ref.dtype)


def _store_slab_t(ref, slab_idx, y):
    yt = y.T
    ref[0, 2 * slab_idx] = yt[:HEAD_DIM].astype(ref.dtype)
    ref[0, 2 * slab_idx + 1] = yt[HEAD_DIM:].astype(ref.dtype)


def _proj_body(x_ref, pos_ref, g_ref, wn_ref, wgt_ref, ws_ref, gq_ref, gks_ref, gkw_ref, freq_ref,
               a_ref, c_ref, vt_ref, s_ref, gt_ref):
    h = _rms_rows(x_ref[...], g_ref[...]).astype(BF16)
    tabs = _rope_tables(pos_ref[0], freq_ref[...])
    scale = HEAD_DIM ** -0.5
    cols = _dot(h, wn_ref[...])
    for j in range(N_NSA_SLABS):
        y = cols[:, LANES * j:LANES * (j + 1)]
        if j < 4:
            y = _slab_rope(_slab_head_norm(y, gq_ref[...]), tabs) * (scale * LOG2E)
            _store_slab(a_ref, j, y)
        elif j == 4 or j == 5:
            _store_slab(c_ref, j - 4, y)
        elif j == 6:
            _store_slab(a_ref, 4, _slab_rope(_slab_head_norm(y, gks_ref[...]), tabs))
        elif j == 8:
            _store_slab(a_ref, 5, _slab_rope(_slab_head_norm(y, gkw_ref[...]), tabs))
        else:
            _store_slab_t(vt_ref, (j - 7) // 2, y)
    gt = jax.nn.sigmoid(_dot(h, wgt_ref[...]))
    gtt = gt.T
    gt_ref[0, 0] = gtt[:GATE_PAD]
    gt_ref[0, 1] = gtt[GATE_PAD:2 * GATE_PAD]
    cols = _dot(h, ws_ref[...])
    for j in range(N_SB_SLABS):
        y = cols[:, LANES * j:LANES * (j + 1)]
        if j < 4:
            y = y * (-scale * LOG2E)
        _store_slab(s_ref, j, y)


def _proj(x, pos3, g, wn, wgt, ws, gq2, gks2, gkw2, freq_lane, tm):
    b, t, d = x.shape
    const = lambda bi, i: (0, 0)
    tok4 = lambda bi, i: (bi, 0, i, 0)
    return pl.pallas_call(
        _proj_body,
        grid=(b, t // tm),
        in_specs=[
            pl.BlockSpec((None, tm, d), lambda bi, i: (bi, i, 0)),
            pl.BlockSpec((1, tm, 1), lambda bi, i: (bi, i, 0)),
            pl.BlockSpec((1, d), const),
            pl.BlockSpec(wn.shape, const, pipeline_mode=pl.Buffered(1)),
            pl.BlockSpec(wgt.shape, const, pipeline_mode=pl.Buffered(1)),
            pl.BlockSpec(ws.shape, const, pipeline_mode=pl.Buffered(1)),
            pl.BlockSpec((1, LANES), const),
            pl.BlockSpec((1, LANES), const),
            pl.BlockSpec((1, LANES), const),
            pl.BlockSpec((1, LANES), const),
        ],
        out_specs=[
            pl.BlockSpec((1, 12, tm, HEAD_DIM), tok4),
            pl.BlockSpec((1, 4, tm, HEAD_DIM), tok4),
            pl.BlockSpec((1, 4, HEAD_DIM, tm), lambda bi, i: (bi, 0, 0, i)),
            pl.BlockSpec((1, 24, tm, HEAD_DIM), tok4),
            pl.BlockSpec((1, 2, GATE_PAD, tm), lambda bi, i: (bi, 0, 0, i)),
        ],
        out_shape=[
            jax.ShapeDtypeStruct((b, 12, t, HEAD_DIM), BF16),
            jax.ShapeDtypeStruct((b, 4, t, HEAD_DIM), BF16),
            jax.ShapeDtypeStruct((b, 4, HEAD_DIM, t), BF16),
            jax.ShapeDtypeStruct((b, 24, t, HEAD_DIM), BF16),
            jax.ShapeDtypeStruct((b, 2, GATE_PAD, t), F32),
        ],
        compiler_params=_cparams(("parallel", "parallel")),
        name="proj",
    )(x, pos3, g, wn, wgt, ws, gq2, gks2, gkw2, freq_lane)


def _compress_body(c_ref, posc_ref, w1_ref, pe_ref, w2a_ref, w2b_ref, gkc_ref, freq_ref, kc_ref, vct_ref):
    for kv in range(2):
        w1 = w1_ref[kv]
        half = w1.shape[0] // 2
        bias = _dot(jnp.broadcast_to(pe_ref[kv], (8, 2 * half)).astype(BF16), w1)[0:1]
        slab = None
        for hh in range(NSA_KV_HEADS):
            chunks = c_ref[0, 2 * kv + hh]
            n_chunks = chunks.shape[0]
            a0 = _dot(chunks, w1[:half])
            a1 = _dot(chunks, w1[half:])
            hid = a0 + pltpu.roll(a1, n_chunks - 1, 0) + bias
            act = jax.nn.gelu(hid).astype(BF16)
            part = _dot(act, (w2a_ref if hh == 0 else w2b_ref)[kv])
            slab = part if slab is None else slab + part
        if kv == 0:
            tabs = _rope_tables(posc_ref[0], freq_ref[...])
            _store_slab(kc_ref, 0, _slab_rope(_slab_head_norm(slab, gkc_ref[...]), tabs))
        else:
            _store_slab_t(vct_ref, 0, slab)


def _compress(cflat, posc3, w1s, pes, w2a, w2b, gkc2, freq_lane):
    b, _, n_chunks, width = cflat.shape
    const = lambda bi: (0, 0)
    whole = lambda arr: pl.BlockSpec(arr.shape, lambda bi: (0,) * arr.ndim)
    return pl.pallas_call(
        _compress_body,
        grid=(b,),
        in_specs=[
            pl.BlockSpec((1, 4, n_chunks, width), lambda bi: (bi, 0, 0, 0)),
            pl.BlockSpec((1, n_chunks, 1), lambda bi: (bi, 0, 0)),
            whole(w1s), whole(pes), whole(w2a), whole(w2b),
            pl.BlockSpec((1, LANES), const),
            pl.BlockSpec((1, LANES), const),
        ],
        out_specs=[
            pl.BlockSpec((1, 2, n_chunks, HEAD_DIM), lambda bi: (bi, 0, 0, 0)),
            pl.BlockSpec((1, 2, HEAD_DIM, n_chunks), lambda bi: (bi, 0, 0, 0)),
        ],
        out_shape=[
            jax.ShapeDtypeStruct((b, 2, n_chunks, HEAD_DIM), BF16),
            jax.ShapeDtypeStruct((b, 2, HEAD_DIM, n_chunks), BF16),
        ],
        compiler_params=_cparams(("parallel",)),
        name="compress",
    )(cflat, posc3, w1s, pes, w2a, w2b, gkc2, freq_lane)


M_INIT = 0.5 * NEG


def _nsa_body(q_ref, kc_ref, vct_ref, ks_ref, vst_ref, kw_ref, vwt_ref, gt_ref, wovt_ref, o_ref,
              s_ref, p_ref, bias_ref, m_ref, l_ref, alpha_ref, acc_ref, *, tq, tk, win_keys, top_n):
    qi = pl.program_id(2)
    s0 = qi * tq
    grp = NSA_GROUP
    cols = grp * tq
    q = q_ref[0].reshape(cols, HEAD_DIM)
    n_cmp = kc_ref.shape[2]
    n_sel = wovt_ref.shape[0]

    t_one = s0 + lax.broadcasted_iota(jnp.int32, (1, tq), 1)
    t_all = jnp.concatenate([t_one] * grp, axis=1)

    sc = _dot_nt(kc_ref[0, 0], q)
    cmp_end = lax.broadcasted_iota(jnp.int32, (n_cmp, 1), 0) * CMP_STRIDE + (CMP_LEN - 1)
    mask_c = cmp_end <= t_all
    sc = jnp.where(mask_c, sc, NEG)
    pc = jnp.where(mask_c, jnp.exp2(sc - jnp.max(sc, axis=0, keepdims=True)), 0.0)
    pc = pc * (1.0 / jnp.maximum(jnp.sum(pc, axis=0, keepdims=True), 1e-30))
    o_c = _dot(vct_ref[0, 0], pc.astype(BF16))

    start = pl.multiple_of(jnp.maximum(s0 + tq - win_keys, 0), tq)
    sw = _dot_nt(kw_ref[0, 0, pl.ds(start, win_keys), :], q)
    diff = t_all - (start + lax.broadcasted_iota(jnp.int32, (win_keys, 1), 0))
    mask_w = (diff >= 0) & (diff < WINDOW)
    sw = jnp.where(mask_w, sw, NEG)
    pw = jnp.where(mask_w, jnp.exp2(sw - jnp.max(sw, axis=0, keepdims=True)), 0.0)
    l_w = jnp.sum(pw, axis=0, keepdims=True)
    o_w = _dot(vwt_ref[0, 0, :, pl.ds(start, win_keys)], pw.astype(BF16)) * (1.0 / jnp.maximum(l_w, 1e-30))
    gates = gt_ref[0, 0]
    gate_rows = [jnp.concatenate([gates[3 * g + c:3 * g + c + 1, :] for g in range(grp)], axis=1) for c in range(3)]
    o_cw = gate_rows[0] * o_c + gate_rows[2] * o_w

    p_sum = pc[:, 0:tq]
    for g in range(1, grp):
        p_sum = p_sum + pc[:, g * tq:(g + 1) * tq]
    p_hi, p_lo = _split_bf16(p_sum)
    imp = _dot(wovt_ref[...], p_hi) + _dot(wovt_ref[...], p_lo)
    jblk = lax.broadcasted_iota(jnp.int32, (n_sel, tq), 0)
    cur = t_one // SEL_BLOCK
    valid = jblk * SEL_BLOCK <= t_one
    forced = (jblk == 0) | (jblk == cur) | (jblk == cur - 1)
    score = jnp.where(valid & forced, FORCE_SCORE, jnp.where(valid, imp, -1.0))
    jf = jblk.astype(F32)
    sel = jnp.zeros((n_sel, tq), F32)
    for _ in range(top_n):
        best = jnp.max(score, axis=0, keepdims=True)
        first = jnp.min(jnp.where(score == best, jf, float(n_sel)), axis=0, keepdims=True)
        pick = jf == first
        sel = jnp.where(pick, 1.0, sel)
        score = jnp.where(pick, -2.0, score)
    bias_ref[...] = jnp.where(valid & (sel > 0.5), 0.0, NEG)

    n_kt = (s0 + tq + tk - 1) // tk
    blocks_per_tile = tk // SEL_BLOCK
    s_ref[...] = jnp.full(s_ref.shape, NEG, F32)
    p_ref[...] = jnp.zeros(p_ref.shape, BF16)
    m_ref[...] = jnp.full(m_ref.shape, M_INIT, F32)
    l_ref[...] = jnp.zeros(l_ref.shape, F32)
    alpha_ref[...] = jnp.ones(alpha_ref.shape, F32)
    acc_ref[...] = jnp.zeros(acc_ref.shape, F32)
    sub_iota = lax.broadcasted_iota(jnp.int32, (tk, 1), 0)

    def step(it, carry):
        kt3 = jnp.clip(it - 2, 0, n_kt - 1)
        v_t = vst_ref[0, 0, :, pl.ds(pl.multiple_of(kt3 * tk, tk), tk)]
        acc_ref[...] = alpha_ref[...] * acc_ref[...] + _dot(v_t, p_ref[...])
        s = s_ref[...]
        m_old = m_ref[...]
        m_new = jnp.maximum(m_old, jnp.max(s, axis=0, keepdims=True))
        alpha = jnp.exp2(m_old - m_new)
        p = jnp.exp2(s - m_new)
        l_ref[...] = alpha * l_ref[...] + jnp.sum(p, axis=0, keepdims=True)
        m_ref[...] = m_new
        alpha_ref[...] = alpha
        p_ref[...] = p.astype(BF16)
        live = it < n_kt
        kt1 = jnp.minimum(it, n_kt - 1)
        rows = [jnp.broadcast_to(bias_ref[pl.ds(kt1 * blocks_per_tile + i, 1), :], (SEL_BLOCK, tq))
                for i in range(blocks_per_tile)]
        kpos = jnp.where(live, kt1 * tk, 2 ** 30) + sub_iota
        bias = jnp.where(kpos <= t_one, jnp.concatenate(rows, axis=0), NEG)
        k = ks_ref[0, 0, pl.ds(pl.multiple_of(kt1 * tk, tk), tk), :]
        s_ref[...] = _dot_nt(k, q) + jnp.concatenate([bias] * grp, axis=1)
        return carry

    lax.fori_loop(0, n_kt + 2, step, 0)
    o_s = acc_ref[...] * (1.0 / jnp.maximum(l_ref[...], 1e-30))

    o = o_cw + gate_rows[1] * o_s
    for pair in range(grp // 2):
        both = jnp.concatenate([o[:, (2 * pair) * tq:(2 * pair + 1) * tq],
                                o[:, (2 * pair + 1) * tq:(2 * pair + 2) * tq]], axis=0)
        o_ref[0, :, 2 * pair * HEAD_DIM:(2 * pair + 2) * HEAD_DIM] = both.T.astype(o_ref.dtype)


def _nsa(a, kc, vct, vt, gates_t, wovt, tk):
    b, _, t, dk = a.shape
    n_cmp = kc.shape[2]
    grp = NSA_GROUP
    tq = LANES
    cols = grp * tq
    top_n = min(SEL_TOPK, t // SEL_BLOCK)
    win_keys = min(t, WINDOW + tq)
    head = lambda base: (lambda bi, h, i: (bi, base + h, 0, 0))
    body = functools.partial(_nsa_body, tq=tq, tk=tk, win_keys=win_keys, top_n=top_n)
    return pl.pallas_call(
        body,
        grid=(b, NSA_KV_HEADS, t // tq),
        in_specs=[
            pl.BlockSpec((1, grp, tq, dk), lambda bi, h, i: (bi, h, i, 0)),
            pl.BlockSpec((1, 1, n_cmp, dk), head(0)),
            pl.BlockSpec((1, 1, dk, n_cmp), head(0)),
            pl.BlockSpec((1, 1, t, dk), head(8)),
            pl.BlockSpec((1, 1, dk, t), head(0)),
            pl.BlockSpec((1, 1, t, dk), head(10)),
            pl.BlockSpec((1, 1, dk, t), head(2)),
            pl.BlockSpec((1, 1, GATE_PAD, tq), lambda bi, h, i: (bi, h, 0, i)),
            pl.BlockSpec(wovt.shape, lambda bi, h, i: (0, 0)),
        ],
        out_specs=pl.BlockSpec((1, tq, grp * dk), lambda bi, h, i: (bi, i, h)),
        out_shape=jax.ShapeDtypeStruct((b, t, NSA_HEADS * dk), BF16),
        scratch_shapes=[
            pltpu.VMEM((tk, cols), F32),
            pltpu.VMEM((tk, cols), BF16),
            pltpu.VMEM((t // SEL_BLOCK, tq), F32),
            pltpu.VMEM((1, cols), F32),
            pltpu.VMEM((1, cols), F32),
            pltpu.VMEM((1, cols), F32),
            pltpu.VMEM((dk, cols), F32),
        ],
        compiler_params=_cparams(("parallel", "parallel", "arbitrary")),
        name="nsa",
    )(a, kc, vct, a, vt, a, vt, gates_t, wovt)


def _sb_body(q_ref, k_ref, v_ref, o_ref, zn_ref, p_ref, c_ref, rs_ref, later_ref, cmr_ref, *, tq):
    qi = pl.program_id(2)
    tk = tq
    masked = -NEG
    rowi = lax.broadcasted_iota(jnp.int32, (tq, tk), 0)
    coli = lax.broadcasted_iota(jnp.int32, (tq, tk), 1)
    later_ref[...] = (rowi > coli).astype(BF16)
    cmr_ref[...] = coli - rowi
    zn_ref[...] = jnp.full(zn_ref.shape, masked, F32)
    p_ref[...] = jnp.full(p_ref.shape, NEG, F32)
    c_ref[...] = jnp.zeros(c_ref.shape, F32)
    rs_ref[...] = jnp.zeros(rs_ref.shape, F32)

    def step(it, acc):
        kb3 = qi - it + 2
        kb1 = qi - it
        off3 = pl.multiple_of(jnp.clip(kb3, 0, qi) * tk, tk)
        off1 = pl.multiple_of(jnp.maximum(kb1, 0) * tk, tk)
        thr = jnp.where(kb1 >= 0, (qi - kb1) * tk, -(2 ** 30))
        mask = cmr_ref[...] < thr
        acc_new = []
        for hh in range(2):
            c_run = c_ref[hh]
            a = jnp.exp2(p_ref[hh] + c_run).astype(BF16)
            acc_new.append(acc[hh] + _dot(a, v_ref[0, hh, pl.ds(off3, tk), :]))
            c_ref[hh] = c_run + rs_ref[hh]
            zn = zn_ref[hh]
            log_rem = jnp.minimum(zn, 0.0) - jnp.log(1.0 + jnp.exp2(-jnp.abs(zn))) * LOG2E
            suffix = _dot(log_rem.astype(BF16), later_ref[...])
            p_ref[hh] = log_rem - zn + suffix
            rs_ref[hh] = jnp.sum(log_rem, axis=-1, keepdims=True)
            zn_ref[hh] = jnp.where(mask, _dot_nt(q_ref[0, hh], k_ref[0, hh, pl.ds(off1, tk), :]), masked)
        return tuple(acc_new)

    zacc = jnp.zeros((tq, HEAD_DIM), F32)
    acc = lax.fori_loop(0, qi + 3, step, (zacc, zacc))
    for hh in range(2):
        o_ref[0, :, hh * HEAD_DIM:(hh + 1) * HEAD_DIM] = acc[hh].astype(o_ref.dtype)


def _sb(s, tq):
    b, _, t, dk = s.shape
    pair = lambda base: (lambda bi, hp, i: (bi, base + hp, 0, 0))
    return pl.pallas_call(
        functools.partial(_sb_body, tq=tq),
        grid=(b, SB_HEADS // 2, t // tq),
        in_specs=[
            pl.BlockSpec((1, 2, tq, dk), lambda bi, hp, i: (bi, hp, i, 0)),
            pl.BlockSpec((1, 2, t, dk), pair(SB_HEADS // 2)),
            pl.BlockSpec((1, 2, t, dk), pair(SB_HEADS)),
        ],
        out_specs=pl.BlockSpec((1, tq, 2 * dk), lambda bi, hp, i: (bi, i, hp)),
        out_shape=jax.ShapeDtypeStruct((b, t, SB_HEADS * dk), BF16),
        scratch_shapes=[pltpu.VMEM((2, tq, tq), F32), pltpu.VMEM((2, tq, tq), F32),
                        pltpu.VMEM((2, tq, 1), F32), pltpu.VMEM((2, tq, 1), F32),
                        pltpu.VMEM((tq, tq), BF16), pltpu.VMEM((tq, tq), jnp.int32)],
        compiler_params=_cparams(("parallel", "parallel", "arbitrary")),
        name="sb",
    )(s, s, s)


def _memkv_body(m_ref, g_ref, wk_ref, wv_ref, gk_ref, k_ref, v_ref):
    mn = _rms_rows(m_ref[0], g_ref[...]).astype(BF16)
    kk = _dot(mn, wk_ref[...])
    vv = _dot(mn, wv_ref[...])
    for j in range(MEM_HEADS // 2):
        ks = _slab_head_norm(kk[:, LANES * j:LANES * (j + 1)], gk_ref[...])
        vs = vv[:, LANES * j:LANES * (j + 1)]
        _store_slab(k_ref, j, ks)
        _store_slab(v_ref, j, vs)


def _memkv(mem, g, wk, wv, gk2):
    b, m, d = mem.shape
    const = lambda bi: (0, 0)
    out = jax.ShapeDtypeStruct((b, MEM_HEADS, m, HEAD_DIM), BF16)
    return pl.pallas_call(
        _memkv_body,
        grid=(b,),
        in_specs=[
            pl.BlockSpec((1, m, d), lambda bi: (bi, 0, 0)),
            pl.BlockSpec((1, d), const),
            pl.BlockSpec(wk.shape, const),
            pl.BlockSpec(wv.shape, const),
            pl.BlockSpec((1, LANES), const),
        ],
        out_specs=[pl.BlockSpec((1, MEM_HEADS, m, HEAD_DIM), lambda bi: (bi, 0, 0, 0))] * 2,
        out_shape=[out, out],
        compiler_params=_cparams(("parallel",)),
        name="memkv",
    )(mem, g, wk, wv, gk2)


def _outmem_body(x_ref, on_ref, os_ref, wo1_ref, wo2_ref, g_ref, wq_ref, gq_ref, mk_ref, mv_ref, wmo_ref, o_ref):
    x = x_ref[0] + _dot(on_ref[0], wo1_ref[...]) + _dot(os_ref[0], wo2_ref[...])
    h = _rms_rows(x, g_ref[...]).astype(BF16)
    qq = _dot(h, wq_ref[...])
    scale = HEAD_DIM ** -0.5
    upd = jnp.zeros_like(x)
    for j in range(MEM_HEADS // 2):
        qs = (_slab_head_norm(qq[:, LANES * j:LANES * (j + 1)], gq_ref[...]) * scale).astype(BF16)
        for hh in range(2):
            head = 2 * j + hh
            s = _dot_nt(qs[:, hh * HEAD_DIM:(hh + 1) * HEAD_DIM], mk_ref[0, head])
            p = jnp.exp(s - jnp.max(s, axis=-1, keepdims=True))
            p = p / jnp.sum(p, axis=-1, keepdims=True)
            o = _dot(p.astype(BF16), mv_ref[0, head]).astype(BF16)
            upd = upd + _dot(o, wmo_ref[head * HEAD_DIM:(head + 1) * HEAD_DIM, :])
    o_ref[0] = x + upd


def _outmem(x, o_nsa, o_sb, wo1, wo2, g, wq, gq2, mk, mv, wmo, tm):
    b, t, d = x.shape
    m = mk.shape[2]
    const = lambda bi, i: (0, 0)
    tok = lambda bi, i: (bi, i, 0)
    return pl.pallas_call(
        _outmem_body,
        grid=(b, t // tm),
        in_specs=[
            pl.BlockSpec((1, tm, d), tok),
            pl.BlockSpec((1, tm, o_nsa.shape[2]), tok),
            pl.BlockSpec((1, tm, o_sb.shape[2]), tok),
            pl.BlockSpec(wo1.shape, const),
            pl.BlockSpec(wo2.shape, const),
            pl.BlockSpec((1, d), const),
            pl.BlockSpec(wq.shape, const),
            pl.BlockSpec((1, LANES), const),
            pl.BlockSpec((1, MEM_HEADS, m, HEAD_DIM), lambda bi, i: (bi, 0, 0, 0)),
            pl.BlockSpec((1, MEM_HEADS, m, HEAD_DIM), lambda bi, i: (bi, 0, 0, 0)),
            pl.BlockSpec(wmo.shape, const),
        ],
        out_specs=pl.BlockSpec((1, tm, d), tok),
        out_shape=jax.ShapeDtypeStruct((b, t, d), F32),
        compiler_params=_cparams(("parallel", "parallel")),
        name="outmem",
    )(x, o_nsa, o_sb, wo1, wo2, g, wq, gq2, mk, mv, wmo)


def _tile2(g):
    return jnp.concatenate([g, g]).reshape(1, LANES).astype(F32)


def _overlap_matrix(n_cmp_pad, n_sel):
    n_cmp = n_cmp_pad - 1
    cs = np.arange(n_cmp_pad)[:, None] * CMP_STRIDE
    ss = np.arange(n_sel)[None, :] * SEL_BLOCK
    ov = np.minimum(cs + CMP_LEN, ss + SEL_BLOCK) - np.maximum(cs, ss)
    ov = np.clip(ov, 0, None).astype(np.float32) / CMP_LEN
    ov[n_cmp:] = 0.0
    return jnp.asarray(ov.T, dtype=BF16)


def _pick_tile(n, pref):
    while n % pref:
        pref //= 2
    return pref


def _layer(x, mem, positions, p):
    b, t, d = x.shape
    n_tok = b * t
    tm = _pick_tile(t, 512)

    x = _ffn(x.reshape(n_tok, d), p["ffn1_norm"].reshape(1, d), p["ffn1_wg"].astype(BF16), p["ffn1_wu"].astype(BF16),
             p["ffn1_wd"].astype(BF16), tm).reshape(b, t, d)

    w_in = p["w_in"]
    n_nsa = N_NSA_SLABS * LANES
    n_gate = NSA_HEADS * 3
    wn = w_in[:, :n_nsa].astype(BF16)
    wg_cols = w_in[:, n_nsa:n_nsa + n_gate].reshape(d, NSA_KV_HEADS, NSA_GROUP * 3)
    wgt = jnp.pad(wg_cols, ((0, 0), (0, 0), (0, GATE_PAD - NSA_GROUP * 3))).reshape(d, NSA_KV_HEADS * GATE_PAD)
    wgt = jnp.pad(wgt, ((0, 0), (0, LANES - NSA_KV_HEADS * GATE_PAD))).astype(BF16)
    ws = w_in[:, n_nsa + n_gate:].astype(BF16)
    freqs = ROPE_THETA ** (-jnp.arange(0, ROT_DIM, 2, dtype=F32) / ROT_DIM)
    half = ROT_DIM // 2
    freq_head = jnp.concatenate([freqs, freqs, jnp.zeros((HEAD_DIM - ROT_DIM,), F32)])
    freq_lane = jnp.concatenate([freq_head, freq_head]).reshape(1, LANES)
    del half
    a, c, vt, s, gates_t = _proj(x, positions.reshape(b, t, 1), p["mix_norm"].reshape(1, d), wn, wgt, ws,
                                 _tile2(p["nsa_q_norm"]), _tile2(p["nsa_ks_norm"]), _tile2(p["nsa_kw_norm"]),
                                 freq_lane, tm)

    n_chunks = t // CMP_STRIDE
    n_cmp = n_chunks - CMP_LEN // CMP_STRIDE + 1
    cflat = c.reshape(b, 4, n_chunks, CMP_STRIDE * HEAD_DIM)
    posc = jnp.pad(positions[:, CMP_LEN - 1::CMP_STRIDE][:, :n_cmp], ((0, 0), (0, n_chunks - n_cmp)))
    w1s = jnp.stack([p["cmp_w1_k"], p["cmp_w1_v"]]).astype(BF16)
    pes = jnp.stack([p["cmp_pos_k"].reshape(1, -1), p["cmp_pos_v"].reshape(1, -1)])
    zpad = ((0, 0), (0, 0), (0, HEAD_DIM))
    w2s = jnp.stack([p["cmp_w2_k"], p["cmp_w2_v"]])
    w2a = jnp.pad(w2s, zpad).astype(BF16)
    w2b = jnp.pad(w2s, ((0, 0), (0, 0), (HEAD_DIM, 0))).astype(BF16)
    kc, vct = _compress(cflat, posc.reshape(b, n_chunks, 1), w1s, pes, w2a, w2b, _tile2(p["nsa_kc_norm"]), freq_lane)

    wovt = _overlap_matrix(n_chunks, t // SEL_BLOCK)
    o_nsa = _nsa(a, kc, vct, vt, gates_t, wovt, _pick_tile(t, 512))
    o_sb = _sb(s, _pick_tile(t, 256))

    w_out = p["w_out"].astype(BF16)
    n_o = NSA_HEADS * HEAD_DIM
    mk, mv = _memkv(mem, p["mem_kv_norm"].reshape(1, d), p["mem_wk"].astype(BF16), p["mem_wv"].astype(BF16),
                    _tile2(p["mem_k_norm"]))
    x = _outmem(x, o_nsa, o_sb, w_out[:n_o], w_out[n_o:], p["mem_x_norm"].reshape(1, d), p["mem_wq"].astype(BF16),
                _tile2(p["mem_q_norm"]), mk, mv, p["mem_wo"].astype(BF16), tm)

    x = _ffn(x.reshape(n_tok, d), p["ffn2_norm"].reshape(1, d), p["ffn2_wg"].astype(BF16), p["ffn2_wu"].astype(BF16),
             p["ffn2_wd"].astype(BF16), tm).reshape(b, t, d)
    return x


def kernel(x, mem, positions, ffn1_norm, ffn1_wg, ffn1_wu, ffn1_wd, mix_norm, w_in, nsa_q_norm, nsa_kc_norm,
           nsa_ks_norm, nsa_kw_norm, cmp_pos_k, cmp_w1_k, cmp_w2_k, cmp_pos_v, cmp_w1_v, cmp_w2_v, w_out,
           mem_x_norm, mem_kv_norm, mem_wq, mem_wk, mem_wv, mem_q_norm, mem_k_norm, mem_wo,
           ffn2_norm, ffn2_wg, ffn2_wu, ffn2_wd):
    names = ("ffn1_norm", "ffn1_wg", "ffn1_wu", "ffn1_wd", "mix_norm", "w_in", "nsa_q_norm", "nsa_kc_norm",
             "nsa_ks_norm", "nsa_kw_norm", "cmp_pos_k", "cmp_w1_k", "cmp_w2_k", "cmp_pos_v", "cmp_w1_v", "cmp_w2_v",
             "w_out", "mem_x_norm", "mem_kv_norm", "mem_wq", "mem_wk", "mem_wv", "mem_q_norm", "mem_k_norm", "mem_wo",
             "ffn2_norm", "ffn2_wg", "ffn2_wu", "ffn2_wd")
    vals = (ffn1_norm, ffn1_wg, ffn1_wu, ffn1_wd, mix_norm, w_in, nsa_q_norm, nsa_kc_norm,
            nsa_ks_norm, nsa_kw_norm, cmp_pos_k, cmp_w1_k, cmp_w2_k, cmp_pos_v, cmp_w1_v, cmp_w2_v,
            w_out, mem_x_norm, mem_kv_norm, mem_wq, mem_wk, mem_wv, mem_q_norm, mem_k_norm, mem_wo,
            ffn2_norm, ffn2_wg, ffn2_wu, ffn2_wd)
    depth = ffn1_norm.shape[0]
    for l in range(depth):
        x = _layer(x, mem, positions, {n: v[l] for n, v in zip(names, vals)})
    return x
```
